```python
import jax, jax.numpy as jnp
from jax import lax
import numpy as np

D_MODEL = 1024
BATCH = 8
SEQ = 2048
DEPTH = 2
DEC_BATCH = 128
DEC_SEQ = 4
PAST_LEN = 16384
PAGE_SIZE = 128

N_EVEN = (DEPTH + 1) // 2
N_ODD = DEPTH // 2
MIX_A = D_MODEL // 2
HEAD_A = 64
N_HEADS_A = MIX_A // HEAD_A
LORA_DECAY = 64
LORA_ICLR = 64
LORA_GATE = 128
PROJ_A = 3 * MIX_A + LORA_DECAY + LORA_ICLR + LORA_GATE
MIX_B = D_MODEL - MIX_A
PROJ_B = 3 * MIX_B
CONV_W = 3
CHUNK = 128
GM_WIDTH = D_MODEL
GM_GROUPS = 8
GM_GROUP_DIM = GM_WIDTH // GM_GROUPS
N_MEM = 256
XA_HEADS = 4
XA_HEAD_DIM = D_MODEL // XA_HEADS
D_FF = 2816
RMS_EPS = 1e-6
LN_EPS = 1e-5
GN_EPS = HEAD_A * 1e-5

kernel_name = 'rwkv7_shortconv_chunkgmlp_macaron_decoder_step'


def _rmsnorm(x, g):
    xf = x.astype(jnp.float32)
    y = xf * lax.rsqrt(jnp.mean(xf * xf, axis=-1, keepdims=True) + RMS_EPS)
    return (y * g.astype(jnp.float32)).astype(x.dtype)


def _layernorm(x, g, b, eps):
    xf = x.astype(jnp.float32)
    mu = jnp.mean(xf, axis=-1, keepdims=True)
    var = jnp.mean(jnp.square(xf - mu), axis=-1, keepdims=True)
    y = (xf - mu) * lax.rsqrt(var + eps) * g.astype(jnp.float32) + b.astype(jnp.float32)
    return y.astype(x.dtype)


def _swiglu(x, wg, wu, wd):
    return (jax.nn.silu(x @ wg) * (x @ wu)) @ wd


def _wkv7(r, w, k, v, a, b, s0):
    def step(s, inp):
        rt, wt, kt, vt, at, bt = inp
        sa = jnp.einsum('bhij,bhj->bhi', s, at)
        s = s * wt[:, :, None, :] + sa[..., None] * bt[:, :, None, :] + vt[..., None] * kt[:, :, None, :]
        return s, jnp.einsum('bhij,bhj->bhi', s, rt)
    xs = tuple(jnp.moveaxis(t.astype(jnp.float32), 1, 0) for t in (r, w, k, v, a, b))
    s, ys = lax.scan(step, s0.astype(jnp.float32), xs)
    return jnp.moveaxis(ys, 0, 1), s


def _rwkv7(p, prev, s0, mu, w0, w2, a0, a2, g2, k_k, k_a, r_k, lnx_w, lnx_b):
    bn, t, _ = p.shape
    p_prev = jnp.concatenate([prev[:, None].astype(p.dtype), p[:, :-1]], axis=1)
    ps = p + mu * (p_prev - p)
    o1, o2, o3 = MIX_A, 2 * MIX_A, 3 * MIX_A
    o4 = o3 + LORA_DECAY
    o5 = o4 + LORA_ICLR
    r, k, v = ps[..., :o1], ps[..., o1:o2], ps[..., o2:o3]
    wd, ad, gd = ps[..., o3:o4], ps[..., o4:o5], ps[..., o5:]
    w = -jax.nn.softplus(-(w0 + jnp.tanh(wd) @ w2)) - 0.5
    decay = jnp.exp(-jnp.exp(w.astype(jnp.float32)))
    a = jax.nn.sigmoid(a0 + ad @ a2)
    g = jax.nn.sigmoid(gd) @ g2
    hs = lambda z: z.reshape(bn, t, N_HEADS_A, HEAD_A)
    kk = hs(k * k_k).astype(jnp.float32)
    kk = kk / jnp.maximum(jnp.sqrt(jnp.sum(kk * kk, axis=-1, keepdims=True)), 1e-12)
    k = k * (1.0 + (a - 1.0) * k_a)
    rh, kh, vh, ah = hs(r), hs(k), hs(v), hs(a)
    y, s = _wkv7(rh, hs(decay), kh, vh, -kk, kk * ah, s0)
    y = _layernorm(y.astype(rh.dtype), lnx_w.reshape(N_HEADS_A, HEAD_A), lnx_b.reshape(N_HEADS_A, HEAD_A), GN_EPS)
    y = y + jnp.sum(rh * kh * r_k, axis=-1, keepdims=True) * vh
    out = y.reshape(bn, t, MIX_A) * g
    return out, p[:, -1], s


def _short_conv(p, buf, conv_w):
    h, bg, cg = p[..., :MIX_B], p[..., MIX_B:2 * MIX_B], p[..., 2 * MIX_B:]
    z = cg * h
    zp = jnp.concatenate([buf.astype(z.dtype), z], axis=1)
    t = z.shape[1]
    y = conv_w[0] * zp[:, 0:t]
    for j in range(1, CONV_W):
        y = y + conv_w[j] * zp[:, j:j + t]
    return bg * y, zp[:, -(CONV_W - 1):]


def _chunk_gmlp(p, ln_w, ln_b, w_s, b_s):
    zg = jax.nn.gelu(p, approximate=False)
    u, v = zg[..., :GM_WIDTH], zg[..., GM_WIDTH:]
    v = _layernorm(v, ln_w, ln_b, LN_EPS)
    bn, t, _ = v.shape
    n_chunks = -(-t // CHUNK)
    pad = n_chunks * CHUNK - t
    vc = jnp.pad(v, ((0, 0), (0, pad), (0, 0))).reshape(bn, n_chunks, CHUNK, GM_GROUPS, GM_GROUP_DIM)
    causal = jnp.tril(jnp.ones((CHUNK, CHUNK), dtype=bool))
    ws = jnp.where(causal[None], w_s, jnp.zeros((), w_s.dtype))
    f = jnp.einsum('gts,bnsgc->bntgc', ws, vc) + jnp.transpose(b_s)[None, None, :, :, None]
    f = f.reshape(bn, n_chunks * CHUNK, GM_WIDTH)[:, :t]
    return u * f, v


def _mem_attn(x, mk, mv, wq, wo):
    bn, t, _ = x.shape
    q = (x @ wq).reshape(bn, t, XA_HEADS, XA_HEAD_DIM)
    sc = jnp.einsum('bthd,bmhd->bhtm', q, mk.astype(q.dtype)).astype(jnp.float32) * (XA_HEAD_DIM ** -0.5)
    pr = jax.nn.softmax(sc, axis=-1).astype(x.dtype)
    o = jnp.einsum('bhtm,bmhd->bthd', pr, mv.astype(x.dtype)).reshape(bn, t, D_MODEL)
    return o @ wo


def _trunk(x, mem_k, mem_v, shift0, wkv0, conv0, P):
    shifts, wkvs, convs, vrows = [], [], [], []
    for l in range(DEPTH):
        n = P['norms'][l]
        x = x + 0.5 * _swiglu(_rmsnorm(x, n[0]), P['f1_wg'][l], P['f1_wu'][l], P['f1_wd'][l])
        h = _rmsnorm(x, n[1])
        if l % 2 == 0:
            e = l // 2
            p = h @ P['w_in_even'][e]
            ya, sh, s = _rwkv7(p[..., :PROJ_A], shift0[e], wkv0[e], P['shift_mu'][e],
                               P['decay_w0'][e], P['decay_w2'][e], P['iclr_a0'][e], P['iclr_a2'][e],
                               P['gate_g2'][e], P['k_k'][e], P['k_a'][e], P['r_k'][e],
                               P['lnx_w'][e], P['lnx_b'][e])
            yb, cb = _short_conv(p[..., PROJ_A:], conv0[e], P['conv_w'][e])
            x = x + jnp.concatenate([ya.astype(x.dtype), yb.astype(x.dtype)], axis=-1) @ P['w_out_even'][e]
            shifts.append(sh)
            wkvs.append(s)
            convs.append(cb)
        else:
            o = l // 2
            p = h @ P['w_in_odd'][o]
            yc, vr = _chunk_gmlp(p, P['gm_ln_w'][o], P['gm_ln_b'][o], P['gm_ws'][o], P['gm_bs'][o])
            x = x + yc @ P['w_out_odd'][o]
            vrows.append(vr)
        x = x + _mem_attn(_rmsnorm(x, n[2]), mem_k[l], mem_v[l], P['xa_wq'][l], P['xa_wo'][l])
        x = x + 0.5 * _swiglu(_rmsnorm(x, n[3]), P['f2_wg'][l], P['f2_wu'][l], P['f2_wd'][l])
    return _rmsnorm(x, P['final_norm']), shifts, wkvs, convs, vrows


def setup_inputs(seed: int = 0) -> dict:
    key = jax.random.key(seed)
    ks = iter(jax.random.split(key, 48))
    def nrm(shape, scale=1.0, offset=0.0):
        return offset + scale * jax.random.normal(next(ks), shape, jnp.float32)
    D = D_MODEL
    inp = {}
    inp['x_prompt'] = nrm((BATCH, SEQ, D))
    inp['x_sample'] = nrm((DEC_BATCH, DEC_SEQ, D))
    inp['mem_prompt'] = nrm((BATCH, N_MEM, D))
    inp['state_shift'] = nrm((N_EVEN, DEC_BATCH, PROJ_A))
    inp['state_wkv'] = nrm((N_EVEN, DEC_BATCH, N_HEADS_A, HEAD_A, HEAD_A), 0.5)
    inp['state_conv'] = nrm((N_EVEN, DEC_BATCH, CONV_W - 1, MIX_B))
    inp['cache_mem_k'] = nrm((DEPTH, DEC_BATCH, N_MEM, XA_HEADS, XA_HEAD_DIM))
    inp['cache_mem_v'] = nrm((DEPTH, DEC_BATCH, N_MEM, XA_HEADS, XA_HEAD_DIM))
    inp['norms'] = nrm((DEPTH, 4, D), 0.01, 1.0)
    inp['final_norm'] = nrm((D,), 0.01, 1.0)
    inp['f1_wg'] = nrm((DEPTH, D, D_FF), D ** -0.5)
    inp['f1_wu'] = nrm((DEPTH, D, D_FF), D ** -0.5)
    inp['f1_wd'] = nrm((DEPTH, D_FF, D), D_FF ** -0.5)
    inp['f2_wg'] = nrm((DEPTH, D, D_FF), D ** -0.5)
    inp['f2_wu'] = nrm((DEPTH, D, D_FF), D ** -0.5)
    inp['f2_wd'] = nrm((DEPTH, D_FF, D), D_FF ** -0.5)
    inp['xa_wq'] = nrm((DEPTH, D, D), D ** -0.5)
    inp['xa_wk'] = nrm((DEPTH, D, D), D ** -0.5)
    inp['xa_wv'] = nrm((DEPTH, D, D), D ** -0.5)
    inp['xa_wo'] = nrm((DEPTH, D, D), D ** -0.5)
    inp['w_in_even'] = nrm((N_EVEN, D, PROJ_A + PROJ_B), D ** -0.5)
    inp['w_out_even'] = nrm((N_EVEN, D, D), D ** -0.5)
    inp['shift_mu'] = jax.random.uniform(next(ks), (N_EVEN, PROJ_A), jnp.float32)
    inp['decay_w0'] = nrm((N_EVEN, MIX_A), 0.5, -2.0)
    inp['decay_w2'] = nrm((N_EVEN, LORA_DECAY, MIX_A), 0.5 * LORA_DECAY ** -0.5)
    inp['iclr_a0'] = nrm((N_EVEN, MIX_A), 0.1)
    inp['iclr_a2'] = nrm((N_EVEN, LORA_ICLR, MIX_A), LORA_ICLR ** -0.5)
    inp['gate_g2'] = nrm((N_EVEN, LORA_GATE, MIX_A), LORA_GATE ** -0.5)
    inp['k_k'] = nrm((N_EVEN, MIX_A), 0.05, 0.85)
    inp['k_a'] = nrm((N_EVEN, MIX_A), 0.05, 1.0)
    inp['r_k'] = nrm((N_EVEN, N_HEADS_A, HEAD_A), 0.1)
    inp['lnx_w'] = nrm((N_EVEN, MIX_A), 0.01, 1.0)
    inp['lnx_b'] = nrm((N_EVEN, MIX_A), 0.01)
    inp['conv_w'] = nrm((N_EVEN, CONV_W, MIX_B), CONV_W ** -0.5)
    inp['w_in_odd'] = nrm((N_ODD, D, 2 * GM_WIDTH), D ** -0.5)
    inp['w_out_odd'] = nrm((N_ODD, GM_WIDTH, D), GM_WIDTH ** -0.5)
    inp['gm_ln_w'] = nrm((N_ODD, GM_WIDTH), 0.01, 1.0)
    inp['gm_ln_b'] = nrm((N_ODD, GM_WIDTH), 0.01)
    inp['gm_ws'] = nrm((N_ODD, GM_GROUPS, CHUNK, CHUNK), CHUNK ** -0.5)
    inp['gm_bs'] = nrm((N_ODD, GM_GROUPS, CHUNK), 0.1)
    return inp


def reference(x_prompt, x_sample, mem_prompt, state_shift, state_wkv, state_conv, cache_mem_k, cache_mem_v,
              norms, final_norm, f1_wg, f1_wu, f1_wd, f2_wg, f2_wu, f2_wd, xa_wq, xa_wk, xa_wv, xa_wo,
              w_in_even, w_out_even, shift_mu, decay_w0, decay_w2, iclr_a0, iclr_a2, gate_g2, k_k, k_a, r_k,
              lnx_w, lnx_b, conv_w, w_in_odd, w_out_odd, gm_ln_w, gm_ln_b, gm_ws, gm_bs):
    P = dict(norms=norms, final_norm=final_norm, f1_wg=f1_wg, f1_wu=f1_wu, f1_wd=f1_wd,
             f2_wg=f2_wg, f2_wu=f2_wu, f2_wd=f2_wd, xa_wq=xa_wq, xa_wo=xa_wo,
             w_in_even=w_in_even, w_out_even=w_out_even, shift_mu=shift_mu, decay_w0=decay_w0,
             decay_w2=decay_w2, iclr_a0=iclr_a0, iclr_a2=iclr_a2, gate_g2=gate_g2, k_k=k_k, k_a=k_a,
             r_k=r_k, lnx_w=lnx_w, lnx_b=lnx_b, conv_w=conv_w, w_in_odd=w_in_odd, w_out_odd=w_out_odd,
             gm_ln_w=gm_ln_w, gm_ln_b=gm_ln_b, gm_ws=gm_ws, gm_bs=gm_bs)
    bp = x_prompt.shape[0]
    new_mem_k_p = jnp.einsum('bmd,lde->lbme', mem_prompt, xa_wk).reshape(DEPTH, bp, N_MEM, XA_HEADS, XA_HEAD_DIM)
    new_mem_v_p = jnp.einsum('bmd,lde->lbme', mem_prompt, xa_wv).reshape(DEPTH, bp, N_MEM, XA_HEADS, XA_HEAD_DIM)
    shift0 = jnp.zeros((N_EVEN, bp, PROJ_A), x_prompt.dtype)
    wkv0 = jnp.zeros((N_EVEN, bp, N_HEADS_A, HEAD_A, HEAD_A), jnp.float32)
    conv0 = jnp.zeros((N_EVEN, bp, CONV_W - 1, MIX_B), x_prompt.dtype)
    y_prompt, p_sh, p_wkv, p_cv, _ = _trunk(x_prompt, new_mem_k_p, new_mem_v_p, shift0, wkv0, conv0, P)
    y_sample, s_sh, s_wkv, s_cv, s_v = _trunk(x_sample, cache_mem_k, cache_mem_v, state_shift, state_wkv,
                                              state_conv, P)
    new_shift_p = jnp.stack(p_sh)
    new_wkv_p = jnp.stack(p_wkv)
    new_conv_p = jnp.stack(p_cv)
    new_shift_s = jnp.stack(s_sh)
    new_wkv_s = jnp.stack(s_wkv)
    new_conv_s = jnp.stack(s_cv)
    new_gmlp_v_s = jnp.stack(s_v)
    return (y_prompt, y_sample, new_shift_p, new_wkv_p, new_conv_p, new_mem_k_p, new_mem_v_p,
            new_shift_s, new_wkv_s, new_conv_s, new_gmlp_v_s)
```

```python
import functools
import math

import jax
import jax.numpy as jnp
from jax import lax
from jax.experimental import pallas as pl
from jax.experimental.pallas import tpu as pltpu

F32 = jnp.float32
BF16 = jnp.bfloat16
HIGHEST = lax.Precision.HIGHEST

D_MODEL = 1024
D_FF = 2816
MIX_A = 512
HEAD_A = 64
N_HEADS_A = MIX_A // HEAD_A
LORA_DECAY = 64
LORA_ICLR = 64
LORA_GATE = 128
PROJ_A = 3 * MIX_A + LORA_DECAY + LORA_ICLR + LORA_GATE
MIX_B = 512
PROJ_B = 3 * MIX_B
GM_WIDTH = 1024
GM_GROUPS = 8
GM_GROUP_DIM = GM_WIDTH // GM_GROUPS
GM_CHUNK = 128
N_MEM = 256
XA_HEADS = 4
XA_HEAD_DIM = D_MODEL // XA_HEADS
RMS_EPS = 1e-6
LN_EPS = 1e-5
GN_EPS = HEAD_A * 1e-5

V7X_VMEM_BYTES = 64 * 1024 * 1024
VMEM_LIMIT_BYTES = V7X_VMEM_BYTES - 8 * 1024 * 1024
SUBLANES = 8

WKV_CHUNK = 64
FFN_TILE_F = 256


def _params(*semantics):
    return pltpu.CompilerParams(dimension_semantics=semantics, vmem_limit_bytes=VMEM_LIMIT_BYTES)


def _rms(x, gain):
    return x * lax.rsqrt(jnp.mean(x * x, axis=-1, keepdims=True) + RMS_EPS) * gain


def _dot(a, b):
    return jnp.dot(a.astype(BF16), b.astype(BF16), preferred_element_type=F32)


def _dot_nt(a, b):
    return lax.dot_general(a.astype(BF16), b.astype(BF16), (((1,), (1,)), ((), ())),
                           preferred_element_type=F32)


def _dot_hi(a, b):
    return jnp.dot(a, b, precision=HIGHEST, preferred_element_type=F32)


def _dot_nt_hi(a, b):
    return lax.dot_general(a, b, (((1,), (1,)), ((), ())), precision=HIGHEST,
                           preferred_element_type=F32)


def _dot_tn_hi(a, b):
    return lax.dot_general(a, b, (((0,), (0,)), ((), ())), precision=HIGHEST,
                           preferred_element_type=F32)


def _mm_kernel(*refs, norm, act, res_scale):
    it = iter(refs)
    x_ref = next(it)
    g_ref = next(it) if norm else None
    w_ref = next(it)
    r_ref = next(it) if res_scale is not None else None
    o_ref = next(it)
    xs_ref = next(it)

    @pl.when(pl.program_id(1) == 0)
    def _():
        x = x_ref[...]
        if norm:
            x = _rms(x, g_ref[...])
        xs_ref[...] = x.astype(BF16)

    acc = jnp.dot(xs_ref[...], w_ref[...].astype(BF16), preferred_element_type=F32)
    if act == "gelu":
        acc = 0.5 * acc * (1.0 + lax.erf(acc * math.sqrt(0.5)))
    if res_scale is not None:
        acc = r_ref[...] + res_scale * acc
    o_ref[...] = acc


def _mm(x, w, *, gain=None, act=None, res=None, res_scale=None, tm=1024, tn=1024):
    m, k = x.shape
    n = w.shape[1]
    tm = min(tm, m)
    tn = min(tn, n)
    assert m % tm == 0 and n % tn == 0
    norm = gain is not None
    if res is None:
        res_scale = None
    in_specs = [pl.BlockSpec((tm, k), lambda i, j: (i, 0))]
    args = [x]
    if norm:
        in_specs.append(pl.BlockSpec((1, k), lambda i, j: (0, 0)))
        args.append(gain.reshape(1, k))
    in_specs.append(pl.BlockSpec((k, tn), lambda i, j: (0, j)))
    args.append(w)
    if res is not None:
        in_specs.append(pl.BlockSpec((tm, tn), lambda i, j: (i, j)))
        args.append(res)
    return pl.pallas_call(
        functools.partial(_mm_kernel, norm=norm, act=act, res_scale=res_scale),
        grid=(m // tm, n // tn),
        in_specs=in_specs,
        out_specs=pl.BlockSpec((tm, tn), lambda i, j: (i, j)),
        out_shape=jax.ShapeDtypeStruct((m, n), F32),
        scratch_shapes=[pltpu.VMEM((tm, k), BF16)],
        compiler_params=_params("parallel", "arbitrary"),
        name="proj",
    )(*args)


def _ffn_kernel(*refs, final_norm):
    it = iter(refs)
    x_ref, g_ref, wg_ref, wu_ref, wd_ref = (next(it) for _ in range(5))
    fg_ref = next(it) if final_norm else None
    o_ref, xs_ref, acc_ref = next(it), next(it), next(it)
    j = pl.program_id(1)

    @pl.when(j == 0)
    def _():
        xs_ref[...] = _rms(x_ref[...], g_ref[...]).astype(BF16)
        acc_ref[...] = jnp.zeros_like(acc_ref)

    xs = xs_ref[...]
    gate = jnp.dot(xs, wg_ref[...].astype(BF16), preferred_element_type=F32)
    up = jnp.dot(xs, wu_ref[...].astype(BF16), preferred_element_type=F32)
    h = (gate * jax.nn.sigmoid(gate)) * up
    acc_ref[...] += jnp.dot(h.astype(BF16), wd_ref[...].astype(BF16), preferred_element_type=F32)

    @pl.when(j == pl.num_programs(1) - 1)
    def _():
        y = x_ref[...] + 0.5 * acc_ref[...]
        if final_norm:
            y = _rms(y, fg_ref[...])
        o_ref[...] = y


def _ffn(x, gain, wg, wu, wd, *, final_gain=None, tm=1024):
    m, d = x.shape
    f = wg.shape[1]
    tm = min(tm, m)
    tf = FFN_TILE_F
    assert m % tm == 0 and f % tf == 0
    final_norm = final_gain is not None
    in_specs = [
        pl.BlockSpec((tm, d), lambda i, j: (i, 0)),
        pl.BlockSpec((1, d), lambda i, j: (0, 0)),
        pl.BlockSpec((d, tf), lambda i, j: (0, j)),
        pl.BlockSpec((d, tf), lambda i, j: (0, j)),
        pl.BlockSpec((tf, d), lambda i, j: (j, 0)),
    ]
    args = [x, gain.reshape(1, d), wg, wu, wd]
    if final_norm:
        in_specs.append(pl.BlockSpec((1, d), lambda i, j: (0, 0)))
        args.append(final_gain.reshape(1, d))
    return pl.pallas_call(
        functools.partial(_ffn_kernel, final_norm=final_norm),
        grid=(m // tm, f // tf),
        in_specs=in_specs,
        out_specs=pl.BlockSpec((tm, d), lambda i, j: (i, 0)),
        out_shape=jax.ShapeDtypeStruct((m, d), F32),
        scratch_shapes=[pltpu.VMEM((tm, d), BF16), pltpu.VMEM((tm, d), F32)],
        compiler_params=_params("parallel", "arbitrary"),
        name="ffn",
    )(*args)


def _even_mixer_kernel(p_ref, shift0_ref, s0_ref, conv0_ref, mu_ref, w0_ref, w2_ref, a0_ref, a2_ref,
                       g2_ref, kk_ref, ka_ref, rk_ref, lnw_ref, lnb_ref, cw_ref,
                       y_ref, shift_out_ref, s_out_ref, conv_out_ref,
                       shift_carry, conv_carry, *, chunk, valid):
    @pl.when(pl.program_id(1) == 0)
    def _():
        shift_carry[...] = shift0_ref[0]
        conv_carry[...] = conv0_ref[0]
        s_out_ref[...] = s0_ref[...]

    p = p_ref[0]
    row = lax.broadcasted_iota(jnp.int32, (chunk, 1), 0)
    live = row < valid

    pa = p[:, :PROJ_A]
    prev = jnp.where(row == 0, shift_carry[...], pltpu.roll(pa, 1, 0))
    ps = pa + mu_ref[...] * (prev - pa)
    last = pa[valid - 1:valid, :]
    shift_carry[...] = last
    shift_out_ref[0] = last

    o1, o2, o3 = MIX_A, 2 * MIX_A, 3 * MIX_A
    o4 = o3 + LORA_DECAY
    o5 = o4 + LORA_ICLR
    r, k, v = ps[:, :o1], ps[:, o1:o2], ps[:, o2:o3]
    wd, ad, gd = ps[:, o3:o4], ps[:, o4:o5], ps[:, o5:]
    z = -(w0_ref[...] + _dot(jnp.tanh(wd), w2_ref[...]))
    softplus = jnp.maximum(z, 0.0) + jnp.log1p(jnp.exp(-jnp.abs(z)))
    log_decay = -jnp.exp(-softplus - 0.5)
    iclr = jax.nn.sigmoid(a0_ref[...] + _dot(ad, a2_ref[...]))
    gate = _dot(jax.nn.sigmoid(gd), g2_ref[...])
    kkv = k * kk_ref[...]
    k = k * (1.0 + (iclr - 1.0) * ka_ref[...])
    log_decay = jnp.where(live, log_decay, 0.0)
    k = jnp.where(live, k, 0.0)
    v = jnp.where(live, v, 0.0)
    kkv = jnp.where(live, kkv, 0.0)

    ti = lax.broadcasted_iota(jnp.int32, (chunk, chunk), 0)
    si = lax.broadcasted_iota(jnp.int32, (chunk, chunk), 1)
    incl = si <= ti
    strict = si < ti
    cum = _dot_hi(incl.astype(F32), log_decay)
    cum_end = cum[chunk - 1:chunk, :]
    e_in = jnp.exp(cum)
    e_in_prev = jnp.exp(cum - log_decay)
    e_out = jnp.exp(-cum)
    e_tail = jnp.exp(cum_end - cum)
    e_end = jnp.exp(cum_end)

    for h in range(N_HEADS_A):
        sl = slice(h * HEAD_A, (h + 1) * HEAD_A)
        kk_h = kkv[:, sl]
        kk_h = kk_h / jnp.maximum(jnp.sqrt(jnp.sum(kk_h * kk_h, axis=-1, keepdims=True)), 1e-12)
        r_h, k_h, v_h = r[:, sl], k[:, sl], v[:, sl]
        a_h = -kk_h
        b_h = kk_h * iclr[:, sl]
        r_in = r_h * e_in[:, sl]
        a_in = a_h * e_in_prev[:, sl]
        b_out = b_h * e_out[:, sl]
        k_out = k_h * e_out[:, sl]
        b_tail = b_h * e_tail[:, sl]
        k_tail = k_h * e_tail[:, sl]

        s0 = s_out_ref[0, h]
        a_ab = jnp.where(strict, _dot_nt_hi(a_in, b_out), 0.0)
        a_ak = jnp.where(strict, _dot_nt_hi(a_in, k_out), 0.0)
        a_rb = jnp.where(incl, _dot_nt_hi(r_in, b_out), 0.0)
        a_rk = jnp.where(incl, _dot_nt_hi(r_in, k_out), 0.0)
        sa = _dot_nt_hi(a_in, s0) + _dot_hi(a_ak, v_h)
        n_pow = a_ab
        span = 1
        while span < chunk:
            sa = sa + _dot_hi(n_pow, sa)
            span *= 2
            if span < chunk:
                n_pow = _dot_hi(n_pow, n_pow)
        y = _dot_nt_hi(r_in, s0) + _dot_hi(a_rb, sa) + _dot_hi(a_rk, v_h)
        s_out_ref[0, h] = s0 * e_end[:, sl] + _dot_tn_hi(sa, b_tail) + _dot_tn_hi(v_h, k_tail)

        mean = jnp.mean(y, axis=-1, keepdims=True)
        var = jnp.mean(jnp.square(y - mean), axis=-1, keepdims=True)
        y = (y - mean) * lax.rsqrt(var + GN_EPS) * lnw_ref[:, sl] + lnb_ref[:, sl]
        y = y + jnp.sum(r_h * k_h * rk_ref[:, sl], axis=-1, keepdims=True) * v_h
        y_ref[0, :, sl] = y * gate[:, sl]

    pb = p[:, PROJ_A:]
    hb, bg, cg = pb[:, :MIX_B], pb[:, MIX_B:2 * MIX_B], pb[:, 2 * MIX_B:]
    zc = cg * hb
    buf0, buf1 = conv_carry[0:1, :], conv_carry[1:2, :]
    z1 = jnp.where(row == 0, buf1, pltpu.roll(zc, 1, 0))
    z2 = jnp.where(row == 0, buf0, jnp.where(row == 1, buf1, pltpu.roll(zc, 2, 0)))
    y_ref[0, :, MIX_A:] = bg * (cw_ref[0:1, :] * z2 + cw_ref[1:2, :] * z1 + cw_ref[2:3, :] * zc)
    tail = zc[valid - 2:valid, :]
    conv_carry[...] = tail
    conv_out_ref[0] = tail


def _even_mixer(p, shift0, s0, conv0, prm, *, chunk, valid):
    b, t, width = p.shape
    row2 = lambda a: a.reshape(1, -1)
    small = [row2(prm["shift_mu"]), row2(prm["decay_w0"]), prm["decay_w2"], row2(prm["iclr_a0"]),
             prm["iclr_a2"], prm["gate_g2"], row2(prm["k_k"]), row2(prm["k_a"]), row2(prm["r_k"]),
             row2(prm["lnx_w"]), row2(prm["lnx_b"]), prm["conv_w"]]
    const = lambda a: pl.BlockSpec(a.shape, lambda i, c: (0,) * a.ndim)
    in_specs = [
        pl.BlockSpec((1, chunk, width), lambda i, c: (i, c, 0)),
        pl.BlockSpec((1, 1, PROJ_A), lambda i, c: (i, 0, 0)),
        pl.BlockSpec((1, N_HEADS_A, HEAD_A, HEAD_A), lambda i, c: (i, 0, 0, 0)),
        pl.BlockSpec((1, 2, MIX_B), lambda i, c: (i, 0, 0)),
    ] + [const(a) for a in small]
    out_specs = [
        pl.BlockSpec((1, chunk, D_MODEL), lambda i, c: (i, c, 0)),
        pl.BlockSpec((1, 1, PROJ_A), lambda i, c: (i, 0, 0)),
        pl.BlockSpec((1, N_HEADS_A, HEAD_A, HEAD_A), lambda i, c: (i, 0, 0, 0)),
        pl.BlockSpec((1, 2, MIX_B), lambda i, c: (i, 0, 0)),
    ]
    out_shape = [
        jax.ShapeDtypeStruct((b, t, D_MODEL), F32),
        jax.ShapeDtypeStruct((b, 1, PROJ_A), F32),
        jax.ShapeDtypeStruct((b, N_HEADS_A, HEAD_A, HEAD_A), F32),
        jax.ShapeDtypeStruct((b, 2, MIX_B), F32),
    ]
    y, shift, state, conv = pl.pallas_call(
        functools.partial(_even_mixer_kernel, chunk=chunk, valid=valid),
        grid=(b, t // chunk),
        in_specs=in_specs,
        out_specs=out_specs,
        out_shape=out_shape,
        scratch_shapes=[pltpu.VMEM((1, PROJ_A), F32), pltpu.VMEM((2, MIX_B), F32)],
        compiler_params=_params("parallel", "arbitrary"),
        name="even_mixer",
    )(p, shift0.reshape(b, 1, PROJ_A), s0, conv0, *small)
    return y, shift.reshape(b, PROJ_A), state, conv


def _gmlp_kernel(p_ref, lnw_ref, lnb_ref, ws_ref, bs_ref, y_ref, v_ref, *, rows, period):
    p = p_ref[...]
    u = p[:, :GM_WIDTH]
    v = p[:, GM_WIDTH:]
    mean = jnp.mean(v, axis=-1, keepdims=True)
    var = jnp.mean(jnp.square(v - mean), axis=-1, keepdims=True)
    v = (v - mean) * lax.rsqrt(var + LN_EPS) * lnw_ref[...] + lnb_ref[...]
    v_ref[...] = v
    ti = lax.broadcasted_iota(jnp.int32, (rows, rows), 0)
    si = lax.broadcasted_iota(jnp.int32, (rows, rows), 1)
    causal = (si <= ti) & (si >= ti - ti % period)
    vb = v.astype(BF16)
    for g in range(GM_GROUPS):
        sl = slice(g * GM_GROUP_DIM, (g + 1) * GM_GROUP_DIM)
        ws = jnp.where(causal, ws_ref[g], 0.0).astype(BF16)
        f = jnp.dot(ws, vb[:, sl], preferred_element_type=F32) + bs_ref[:, g:g + 1]
        y_ref[:, sl] = u[:, sl] * f


def _gmlp(p, ln_w, ln_b, ws, bs, *, period):
    m = p.shape[0]
    rows = ws.shape[1]
    assert m % rows == 0 and rows % period == 0
    y, v = pl.pallas_call(
        functools.partial(_gmlp_kernel, rows=rows, period=period),
        grid=(m // rows,),
        in_specs=[
            pl.BlockSpec((rows, 2 * GM_WIDTH), lambda i: (i, 0)),
            pl.BlockSpec((1, GM_WIDTH), lambda i: (0, 0)),
            pl.BlockSpec((1, GM_WIDTH), lambda i: (0, 0)),
            pl.BlockSpec((GM_GROUPS, rows, rows), lambda i: (0, 0, 0)),
            pl.BlockSpec((rows, GM_GROUPS), lambda i: (0, 0)),
        ],
        out_specs=[pl.BlockSpec((rows, GM_WIDTH), lambda i: (i, 0)),
                   pl.BlockSpec((rows, GM_WIDTH), lambda i: (i, 0))],
        out_shape=[jax.ShapeDtypeStruct((m, GM_WIDTH), F32), jax.ShapeDtypeStruct((m, GM_WIDTH), F32)],
        compiler_params=_params("parallel"),
        name="gmlp",
    )(p, ln_w.reshape(1, GM_WIDTH), ln_b.reshape(1, GM_WIDTH), ws, bs)
    return y, v


def _attn_kernel(q_ref, k_ref, v_ref, o_ref):
    q = q_ref[0].astype(BF16)
    k = k_ref[0].astype(BF16)
    v = v_ref[0].astype(BF16)
    for h in range(XA_HEADS):
        sl = slice(h * XA_HEAD_DIM, (h + 1) * XA_HEAD_DIM)
        sc = lax.dot_general(q[:, sl], k[:, sl], (((1,), (1,)), ((), ())),
                             preferred_element_type=F32) * (XA_HEAD_DIM ** -0.5)
        e = jnp.exp(sc - jnp.max(sc, axis=-1, keepdims=True))
        pr = e / jnp.sum(e, axis=-1, keepdims=True)
        o_ref[0, :, sl] = jnp.dot(pr.astype(BF16), v[:, sl], preferred_element_type=F32)


def _attn(q, mk, mv, *, tq=512):
    b, t, d = q.shape
    tq = min(tq, t)
    assert t % tq == 0
    return pl.pallas_call(
        _attn_kernel,
        grid=(b, t // tq),
        in_specs=[pl.BlockSpec((1, tq, d), lambda i, j: (i, j, 0)),
                  pl.BlockSpec((1, N_MEM, d), lambda i, j: (i, 0, 0)),
                  pl.BlockSpec((1, N_MEM, d), lambda i, j: (i, 0, 0))],
        out_specs=pl.BlockSpec((1, tq, d), lambda i, j: (i, j, 0)),
        out_shape=jax.ShapeDtypeStruct((b, t, d), F32),
        compiler_params=_params("parallel", "arbitrary"),
        name="mem_attn",
    )(q, mk, mv)


def _trunk(x, mem_k, mem_v, shift0, wkv0, conv0, P, *, pad_t):
    b, t, d = x.shape
    m = b * t
    depth = P["norms"].shape[0]
    x = x.reshape(m, d)
    shifts, wkvs, convs, vrows = [], [], [], []

    def pad_rows(a):
        a = a.reshape(b, t, a.shape[-1])
        return a if pad_t == t else jnp.pad(a, ((0, 0), (0, pad_t - t), (0, 0)))

    def unpad_rows(a):
        return a[:, :t].reshape(m, a.shape[-1])

    for l in range(depth):
        n = P["norms"][l]
        x = _ffn(x, n[0], P["f1_wg"][l], P["f1_wu"][l], P["f1_wd"][l])
        if l % 2 == 0:
            e = l // 2
            p = _mm(x, P["w_in_even"][e], gain=n[1], tm=512, tn=(PROJ_A + PROJ_B) // 2)
            prm = {key: P[key][e] for key in ("shift_mu", "decay_w0", "decay_w2", "iclr_a0", "iclr_a2",
                                              "gate_g2", "k_k", "k_a", "r_k", "lnx_w", "lnx_b", "conv_w")}
            chunk = min(WKV_CHUNK, pad_t)
            y, sh, s, cb = _even_mixer(pad_rows(p), shift0[e], wkv0[e], conv0[e], prm,
                                       chunk=chunk, valid=min(chunk, t))
            x = _mm(unpad_rows(y), P["w_out_even"][e], res=x, res_scale=1.0)
            shifts.append(sh)
            wkvs.append(s)
            convs.append(cb)
        else:
            o = l // 2
            p = _mm(x, P["w_in_odd"][o], gain=n[1], act="gelu")
            period = min(GM_CHUNK, t)
            reps = GM_CHUNK // period
            ws = jnp.tile(P["gm_ws"][o][:, :period, :period], (1, reps, reps))
            bs = jnp.tile(P["gm_bs"][o][:, :period], (1, reps)).T
            yc, vr = _gmlp(p, P["gm_ln_w"][o], P["gm_ln_b"][o], ws, bs, period=period)
            x = _mm(yc, P["w_out_odd"][o], res=x, res_scale=1.0)
            vrows.append(vr.reshape(b, t, GM_WIDTH))
        q = _mm(x, P["xa_wq"][l], gain=n[2])
        att = _attn(pad_rows(q), mem_k[l].reshape(b, N_MEM, d), mem_v[l].reshape(b, N_MEM, d))
        x = _mm(unpad_rows(att), P["xa_wo"][l], res=x, res_scale=1.0)
        x = _ffn(x, n[3], P["f2_wg"][l], P["f2_wu"][l], P["f2_wd"][l],
                 final_gain=P["final_norm"] if l == depth - 1 else None)
    return x.reshape(b, t, d), shifts, wkvs, convs, vrows


def kernel(x_prompt, x_sample, mem_prompt, state_shift, state_wkv, state_conv, cache_mem_k, cache_mem_v,
           norms, final_norm, f1_wg, f1_wu, f1_wd, f2_wg, f2_wu, f2_wd, xa_wq, xa_wk, xa_wv, xa_wo,
           w_in_even, w_out_even, shift_mu, decay_w0, decay_w2, iclr_a0, iclr_a2, gate_g2, k_k, k_a, r_k,
           lnx_w, lnx_b, conv_w, w_in_odd, w_out_odd, gm_ln_w, gm_ln_b, gm_ws, gm_bs):
    P = dict(norms=norms, final_norm=final_norm, f1_wg=f1_wg, f1_wu=f1_wu, f1_wd=f1_wd,
             f2_wg=f2_wg, f2_wu=f2_wu, f2_wd=f2_wd, xa_wq=xa_wq, xa_wo=xa_wo,
             w_in_even=w_in_even, w_out_even=w_out_even, shift_mu=shift_mu, decay_w0=decay_w0,
             decay_w2=decay_w2, iclr_a0=iclr_a0, iclr_a2=iclr_a2, gate_g2=gate_g2, k_k=k_k, k_a=k_a,
             r_k=r_k, lnx_w=lnx_w, lnx_b=lnx_b, conv_w=conv_w, w_in_odd=w_in_odd, w_out_odd=w_out_odd,
             gm_ln_w=gm_ln_w, gm_ln_b=gm_ln_b, gm_ws=gm_ws, gm_bs=gm_bs)
    depth = norms.shape[0]
    n_even = w_in_even.shape[0]
    bp, tp, d = x_prompt.shape
    bs_, ts, _ = x_sample.shape

    mem = mem_prompt.reshape(bp * N_MEM, d)
    new_k = jnp.stack([_mm(mem, xa_wk[l]) for l in range(depth)])
    new_v = jnp.stack([_mm(mem, xa_wv[l]) for l in range(depth)])
    new_mem_k_p = new_k.reshape(depth, bp, N_MEM, XA_HEADS, XA_HEAD_DIM)
    new_mem_v_p = new_v.reshape(depth, bp, N_MEM, XA_HEADS, XA_HEAD_DIM)

    shift0 = jnp.zeros((n_even, bp, PROJ_A), F32)
    wkv0 = jnp.zeros((n_even, bp, N_HEADS_A, HEAD_A, HEAD_A), F32)
    conv0 = jnp.zeros((n_even, bp, 2, MIX_B), F32)
    y_prompt, p_sh, p_wkv, p_cv, _ = _trunk(x_prompt, new_mem_k_p, new_mem_v_p, shift0, wkv0, conv0, P,
                                            pad_t=tp)
    pad_ts = -(-ts // SUBLANES) * SUBLANES
    y_sample, s_sh, s_wkv, s_cv, s_v = _trunk(x_sample, cache_mem_k, cache_mem_v, state_shift, state_wkv,
                                              state_conv, P, pad_t=pad_ts)
    return (y_prompt, y_sample, jnp.stack(p_sh), jnp.stack(p_wkv), jnp.stack(p_cv), new_mem_k_p, new_mem_v_p,
            jnp.stack(s_sh), jnp.stack(s_wkv), jnp.stack(s_cv), jnp.stack(s_v))
```

```python
import functools
import math

import jax
import jax.numpy as jnp
from jax import lax
from jax.experimental import pallas as pl
from jax.experimental.pallas import tpu as pltpu

F32 = jnp.float32
BF16 = jnp.bfloat16

D_MODEL = 1024
D_FF = 2816
MIX_A = 512
HEAD_A = 64
N_HEADS_A = MIX_A // HEAD_A
LORA_DECAY = 64
LORA_ICLR = 64
LORA_GATE = 128
PROJ_A = 3 * MIX_A + LORA_DECAY + LORA_ICLR + LORA_GATE
MIX_B = 512
PROJ_B = 3 * MIX_B
GM_WIDTH = 1024
GM_GROUPS = 8
GM_GROUP_DIM = GM_WIDTH // GM_GROUPS
GM_CHUNK = 128
N_MEM = 256
XA_HEADS = 4
XA_HEAD_DIM = D_MODEL // XA_HEADS
RMS_EPS = 1e-6
LN_EPS = 1e-5
GN_EPS = HEAD_A * 1e-5

V7X_VMEM_BYTES = 64 * 1024 * 1024
VMEM_LIMIT_BYTES = V7X_VMEM_BYTES - 8 * 1024 * 1024
SUBLANES = 8

WKV_CHUNK = 64
FFN_TILE_F = 256


def _params(*semantics):
    return pltpu.CompilerParams(dimension_semantics=semantics, vmem_limit_bytes=VMEM_LIMIT_BYTES)


def _rms(x, gain):
    return x * lax.rsqrt(jnp.mean(x * x, axis=-1, keepdims=True) + RMS_EPS) * gain


def _dot(a, b):
    return jnp.dot(a.astype(BF16), b.astype(BF16), preferred_element_type=F32)


def _mm_kernel(*refs, norm, act, res_scale):
    it = iter(refs)
    x_ref = next(it)
    g_ref = next(it) if norm else None
    w_ref = next(it)
    r_ref = next(it) if res_scale is not None else None
    o_ref = next(it)
    xs_ref = next(it)

    @pl.when(pl.program_id(1) == 0)
    def _():
        x = x_ref[...]
        if norm:
            x = _rms(x, g_ref[...])
        xs_ref[...] = x.astype(BF16)

    acc = jnp.dot(xs_ref[...], w_ref[...].astype(BF16), preferred_element_type=F32)
    if act == "gelu":
        acc = 0.5 * acc * (1.0 + lax.erf(acc * math.sqrt(0.5)))
    if res_scale is not None:
        acc = r_ref[...] + res_scale * acc
    o_ref[...] = acc


def _mm(x, w, layer, *, gain=None, act=None, res=None, res_scale=None, tm=1024, tn=1024):
    m, k = x.shape
    n = w.shape[2]
    tm = min(tm, m)
    tn = min(tn, n)
    assert m % tm == 0 and n % tn == 0
    norm = gain is not None
    if res is None:
        res_scale = None
    in_specs = [pl.BlockSpec((tm, k), lambda i, j: (i, 0))]
    args = [x]
    if norm:
        in_specs.append(pl.BlockSpec((1, k), lambda i, j: (0, 0)))
        args.append(gain.reshape(1, k))
    in_specs.append(pl.BlockSpec((None, k, tn), lambda i, j: (layer, 0, j)))
    args.append(w)
    if res is not None:
        in_specs.append(pl.BlockSpec((tm, tn), lambda i, j: (i, j)))
        args.append(res)
    return pl.pallas_call(
        functools.partial(_mm_kernel, norm=norm, act=act, res_scale=res_scale),
        grid=(m // tm, n // tn),
        in_specs=in_specs,
        out_specs=pl.BlockSpec((tm, tn), lambda i, j: (i, j)),
        out_shape=jax.ShapeDtypeStruct((m, n), F32),
        scratch_shapes=[pltpu.VMEM((tm, k), BF16)],
        compiler_params=_params("parallel", "arbitrary"),
        name="proj",
    )(*args)


def _ffn_kernel(*refs, final_norm):
    it = iter(refs)
    x_ref, g_ref, wg_ref, wu_ref, wd_ref = (next(it) for _ in range(5))
    fg_ref = next(it) if final_norm else None
    o_ref, xs_ref, acc_ref = next(it), next(it), next(it)
    j = pl.program_id(1)

    @pl.when(j == 0)
    def _():
        xs_ref[...] = _rms(x_ref[...], g_ref[...]).astype(BF16)
        acc_ref[...] = jnp.zeros_like(acc_ref)

    xs = xs_ref[...]
    gate = jnp.dot(xs, wg_ref[...].astype(BF16), preferred_element_type=F32)
    up = jnp.dot(xs, wu_ref[...].astype(BF16), preferred_element_type=F32)
    h = (gate * jax.nn.sigmoid(gate)) * up
    acc_ref[...] += jnp.dot(h.astype(BF16), wd_ref[...].astype(BF16), preferred_element_type=F32)

    @pl.when(j == pl.num_programs(1) - 1)
    def _():
        y = x_ref[...] + 0.5 * acc_ref[...]
        if final_norm:
            y = _rms(y, fg_ref[...])
        o_ref[...] = y


def _ffn(x, gain, wg, wu, wd, layer, *, final_gain=None, tm=1024):
    m, d = x.shape
    f = wg.shape[2]
    tm = min(tm, m)
    tf = FFN_TILE_F
    assert m % tm == 0 and f % tf == 0
    final_norm = final_gain is not None
    in_specs = [
        pl.BlockSpec((tm, d), lambda i, j: (i, 0)),
        pl.BlockSpec((1, d), lambda i, j: (0, 0)),
        pl.BlockSpec((None, d, tf), lambda i, j: (layer, 0, j)),
        pl.BlockSpec((None, d, tf), lambda i, j: (layer, 0, j)),
        pl.BlockSpec((None, tf, d), lambda i, j: (layer, j, 0)),
    ]
    args = [x, gain.reshape(1, d), wg, wu, wd]
    if final_norm:
        in_specs.append(pl.BlockSpec((1, d), lambda i, j: (0, 0)))
        args.append(final_gain.reshape(1, d))
    return pl.pallas_call(
        functools.partial(_ffn_kernel, final_norm=final_norm),
        grid=(m // tm, f // tf),
        in_specs=in_specs,
        out_specs=pl.BlockSpec((tm, d), lambda i, j: (i, 0)),
        out_shape=jax.ShapeDtypeStruct((m, d), F32),
        scratch_shapes=[pltpu.VMEM((tm, d), BF16), pltpu.VMEM((tm, d), F32)],
        compiler_params=_params("parallel", "arbitrary"),
        name="ffn",
    )(*args)


NN = ((1,), (0,))
NT = ((1,), (1,))
TN = ((0,), (0,))


def _dotg(a, b, dims):
    return lax.dot_general(a.astype(BF16), b.astype(BF16), (dims, ((), ())), preferred_element_type=F32)


def _cumsum_rows(x, rows):
    row = lax.broadcasted_iota(jnp.int32, (rows, 1), 0)
    d = 1
    while d < rows:
        x = x + jnp.where(row >= d, pltpu.roll(x, d, 0), 0.0)
        d *= 2
    return x


def _even_mixer_kernel(p_ref, shift0_ref, s0_ref, conv0_ref, mu_ref, w0_ref, w2_ref, a0_ref, a2_ref,
                       g2_ref, kk_ref, ka_ref, rk_ref, lnw_ref, lnb_ref, cw_ref,
                       y_ref, shift_out_ref, s_out_ref, conv_out_ref,
                       shift_carry, conv_carry, *, chunk, valid, nb):
    @pl.when(pl.program_id(1) == 0)
    def _():
        shift_carry[...] = shift0_ref[...]
        conv_carry[...] = conv0_ref[...]
        s_out_ref[...] = s0_ref[...]

    row = lax.broadcasted_iota(jnp.int32, (chunk, 1), 0)
    live = row < valid
    ti = lax.broadcasted_iota(jnp.int32, (chunk, chunk), 0)
    si = lax.broadcasted_iota(jnp.int32, (chunk, chunk), 1)
    incl = si <= ti
    strict = si < ti
    o1, o2, o3 = MIX_A, 2 * MIX_A, 3 * MIX_A
    o4 = o3 + LORA_DECAY
    o5 = o4 + LORA_ICLR

    def one_sequence(bi, carry):
        p = p_ref[bi]

        pa = p[:, :PROJ_A]
        prev = jnp.where(row == 0, shift_carry[bi], pltpu.roll(pa, 1, 0))
        ps = pa + mu_ref[...] * (prev - pa)
        last = pa[valid - 1:valid, :]
        shift_carry[bi] = last
        shift_out_ref[bi] = last

        r, k, v = ps[:, :o1], ps[:, o1:o2], ps[:, o2:o3]
        wd, ad, gd = ps[:, o3:o4], ps[:, o4:o5], ps[:, o5:]
        z = -(w0_ref[...] + _dot(jnp.tanh(wd), w2_ref[...]))
        softplus = jnp.maximum(z, 0.0) + jnp.log1p(jnp.exp(-jnp.abs(z)))
        log_decay = -jnp.exp(-softplus - 0.5)
        iclr = jax.nn.sigmoid(a0_ref[...] + _dot(ad, a2_ref[...]))
        gate = _dot(jax.nn.sigmoid(gd), g2_ref[...])
        kkv = k * kk_ref[...]
        k = k * (1.0 + (iclr - 1.0) * ka_ref[...])
        if valid < chunk:
            log_decay = jnp.where(live, log_decay, 0.0)
            k = jnp.where(live, k, 0.0)
            v = jnp.where(live, v, 0.0)
            kkv = jnp.where(live, kkv, 0.0)

        cum = _cumsum_rows(log_decay, chunk)
        cum_end = cum[chunk - 1:chunk, :]
        e_in = jnp.exp(cum)
        e_in_prev = jnp.exp(cum - log_decay)
        e_out = jnp.exp(-cum)
        e_tail = jnp.exp(cum_end - cum)
        e_end = jnp.exp(cum_end)
        r_in = r * e_in
        k_out = k * e_out
        k_tail = k * e_tail
        rk_bonus = r * k * rk_ref[...]

        heads = range(N_HEADS_A)
        sls = [slice(h * HEAD_A, (h + 1) * HEAD_A) for h in heads]
        s0 = [s_out_ref[bi, h] for h in heads]
        v_h, ar, b_out, k_out_h, tails = [], [], [], [], []
        for sl in sls:
            kk_h = kkv[:, sl]
            kk_h = kk_h / jnp.maximum(jnp.sqrt(jnp.sum(kk_h * kk_h, axis=-1, keepdims=True)), 1e-12)
            b_h = kk_h * iclr[:, sl]
            v_h.append(v[:, sl])
            ar.append(jnp.concatenate([-kk_h * e_in_prev[:, sl], r_in[:, sl]], axis=0))
            b_out.append(b_h * e_out[:, sl])
            k_out_h.append(k_out[:, sl])
            tails.append(jnp.concatenate([b_h * e_tail[:, sl], k_tail[:, sl]], axis=0))
        mask2 = jnp.concatenate([strict, incl], axis=0)
        g_b = [jnp.where(mask2, _dotg(ar[h], b_out[h], NT), 0.0) for h in heads]
        g_k = [jnp.where(mask2, _dotg(ar[h], k_out_h[h], NT), 0.0) for h in heads]
        from_state = [_dotg(ar[h], s0[h], NT) for h in heads]
        from_v = [_dotg(g_k[h], v_h[h], NN) for h in heads]
        sa = [from_state[h][:chunk] + from_v[h][:chunk] for h in heads]
        n_pow = [g_b[h][:chunk] for h in heads]
        span = 1
        while span < chunk:
            sa = [sa[h] + _dotg(n_pow[h], sa[h], NN) for h in heads]
            span *= 2
            if span < chunk:
                n_pow = [_dotg(n_pow[h], n_pow[h], NN) for h in heads]
        ys = [from_state[h][chunk:] + from_v[h][chunk:] + _dotg(g_b[h][chunk:], sa[h], NN) for h in heads]
        new_states = [s0[h] * e_end[:, sls[h]]
                      + _dotg(jnp.concatenate([sa[h], v_h[h]], axis=0), tails[h], TN) for h in heads]
        for h in heads:
            s_out_ref[bi, h] = new_states[h]
        for h, sl in enumerate(sls):
            y = ys[h]
            mean = jnp.mean(y, axis=-1, keepdims=True)
            var = jnp.mean(jnp.square(y - mean), axis=-1, keepdims=True)
            y = (y - mean) * lax.rsqrt(var + GN_EPS) * lnw_ref[:, sl] + lnb_ref[:, sl]
            y = y + jnp.sum(rk_bonus[:, sl], axis=-1, keepdims=True) * v_h[h]
            y_ref[bi, :, sl] = y * gate[:, sl]

        pb = p[:, PROJ_A:]
        hb, bg, cg = pb[:, :MIX_B], pb[:, MIX_B:2 * MIX_B], pb[:, 2 * MIX_B:]
        zc = cg * hb
        buf = conv_carry[bi]
        buf0, buf1 = buf[0:1, :], buf[1:2, :]
        z1 = jnp.where(row == 0, buf1, pltpu.roll(zc, 1, 0))
        z2 = jnp.where(row == 0, buf0, jnp.where(row == 1, buf1, pltpu.roll(zc, 2, 0)))
        y_ref[bi, :, MIX_A:] = bg * (cw_ref[0:1, :] * z2 + cw_ref[1:2, :] * z1 + cw_ref[2:3, :] * zc)
        tail = zc[valid - 2:valid, :]
        conv_carry[bi] = tail
        conv_out_ref[bi] = tail
        return carry

    if nb == 1:
        one_sequence(0, 0)
    else:
        lax.fori_loop(0, nb, one_sequence, 0)


def _even_mixer(p, shift0, s0, conv0, prm, *, chunk, valid, nb):
    b, t, width = p.shape
    assert b % nb == 0 and t % chunk == 0
    row2 = lambda a: a.reshape(1, -1)
    small = [row2(prm["shift_mu"]), row2(prm["decay_w0"]), prm["decay_w2"], row2(prm["iclr_a0"]),
             prm["iclr_a2"], prm["gate_g2"], row2(prm["k_k"]), row2(prm["k_a"]), row2(prm["r_k"]),
             row2(prm["lnx_w"]), row2(prm["lnx_b"]), prm["conv_w"]]
    const = lambda a: pl.BlockSpec(a.shape, lambda i, c: (0,) * a.ndim)
    in_specs = [
        pl.BlockSpec((nb, chunk, width), lambda i, c: (i, c, 0)),
        pl.BlockSpec((nb, 1, PROJ_A), lambda i, c: (i, 0, 0)),
        pl.BlockSpec((nb, N_HEADS_A, HEAD_A, HEAD_A), lambda i, c: (i, 0, 0, 0)),
        pl.BlockSpec((nb, 2, MIX_B), lambda i, c: (i, 0, 0)),
    ] + [const(a) for a in small]
    out_specs = [
        pl.BlockSpec((nb, chunk, D_MODEL), lambda i, c: (i, c, 0)),
        pl.BlockSpec((nb, 1, PROJ_A), lambda i, c: (i, 0, 0)),
        pl.BlockSpec((nb, N_HEADS_A, HEAD_A, HEAD_A), lambda i, c: (i, 0, 0, 0)),
        pl.BlockSpec((nb, 2, MIX_B), lambda i, c: (i, 0, 0)),
    ]
    out_shape = [
        jax.ShapeDtypeStruct((b, t, D_MODEL), F32),
        jax.ShapeDtypeStruct((b, 1, PROJ_A), F32),
        jax.ShapeDtypeStruct((b, N_HEADS_A, HEAD_A, HEAD_A), F32),
        jax.ShapeDtypeStruct((b, 2, MIX_B), F32),
    ]
    y, shift, state, conv = pl.pallas_call(
        functools.partial(_even_mixer_kernel, chunk=chunk, valid=valid, nb=nb),
        grid=(b // nb, t // chunk),
        in_specs=in_specs,
        out_specs=out_specs,
        out_shape=out_shape,
        scratch_shapes=[pltpu.VMEM((nb, 1, PROJ_A), F32), pltpu.VMEM((nb, 2, MIX_B), F32)],
        compiler_params=_params("parallel", "arbitrary"),
        name="even_mixer",
    )(p, shift0.reshape(b, 1, PROJ_A), s0, conv0, *small)
    return y, shift.reshape(b, PROJ_A), state, conv


def _gmlp_kernel(p_ref, lnw_ref, lnb_ref, ws_ref, bs_ref, y_ref, v_ref, *, rows, period):
    p = p_ref[...]
    u = p[:, :GM_WIDTH]
    v = p[:, GM_WIDTH:]
    mean = jnp.mean(v, axis=-1, keepdims=True)
    var = jnp.mean(jnp.square(v - mean), axis=-1, keepdims=True)
    v = (v - mean) * lax.rsqrt(var + LN_EPS) * lnw_ref[...] + lnb_ref[...]
    v_ref[...] = v
    ti = lax.broadcasted_iota(jnp.int32, (rows, rows), 0)
    si = lax.broadcasted_iota(jnp.int32, (rows, rows), 1)
    causal = (si <= ti) & (si >= ti - ti % period)
    vb = v.astype(BF16)
    for g in range(GM_GROUPS):
        sl = slice(g * GM_GROUP_DIM, (g + 1) * GM_GROUP_DIM)
        ws = jnp.where(causal, ws_ref[g], 0.0).astype(BF16)
        f = jnp.dot(ws, vb[:, sl], preferred_element_type=F32) + bs_ref[:, g:g + 1]
        y_ref[:, sl] = u[:, sl] * f


def _gmlp(p, ln_w, ln_b, ws, bs, *, period):
    m = p.shape[0]
    rows = ws.shape[1]
    assert m % rows == 0 and rows % period == 0
    y, v = pl.pallas_call(
        functools.partial(_gmlp_kernel, rows=rows, period=period),
        grid=(m // rows,),
        in_specs=[
            pl.BlockSpec((rows, 2 * GM_WIDTH), lambda i: (i, 0)),
            pl.BlockSpec((1, GM_WIDTH), lambda i: (0, 0)),
            pl.BlockSpec((1, GM_WIDTH), lambda i: (0, 0)),
            pl.BlockSpec((GM_GROUPS, rows, rows), lambda i: (0, 0, 0)),
            pl.BlockSpec((rows, GM_GROUPS), lambda i: (0, 0)),
        ],
        out_specs=[pl.BlockSpec((rows, GM_WIDTH), lambda i: (i, 0)),
                   pl.BlockSpec((rows, GM_WIDTH), lambda i: (i, 0))],
        out_shape=[jax.ShapeDtypeStruct((m, GM_WIDTH), F32), jax.ShapeDtypeStruct((m, GM_WIDTH), F32)],
        compiler_params=_params("parallel"),
        name="gmlp",
    )(p, ln_w.reshape(1, GM_WIDTH), ln_b.reshape(1, GM_WIDTH), ws, bs)
    return y, v


def _attn_kernel(q_ref, k_ref, v_ref, o_ref, *, heads_minor):
    q = q_ref[0].astype(BF16)
    for h in range(XA_HEADS):
        sl = slice(h * XA_HEAD_DIM, (h + 1) * XA_HEAD_DIM)
        if heads_minor:
            k_h, v_h = k_ref[:, h, :], v_ref[:, h, :]
        else:
            k_h, v_h = k_ref[:, sl], v_ref[:, sl]
        sc = lax.dot_general(q[:, sl], k_h.astype(BF16), (((1,), (1,)), ((), ())),
                             preferred_element_type=F32) * (XA_HEAD_DIM ** -0.5)
        e = jnp.exp(sc - jnp.max(sc, axis=-1, keepdims=True))
        pr = e / jnp.sum(e, axis=-1, keepdims=True)
        o_ref[0, :, sl] = jnp.dot(pr.astype(BF16), v_h.astype(BF16), preferred_element_type=F32)


def _attn(q, mk, mv, layer, *, tq=512):
    b, t, d = q.shape
    tq = min(tq, t)
    assert t % tq == 0
    heads_minor = mk.ndim == 5
    if heads_minor:
        kv_spec = pl.BlockSpec((None, None, N_MEM, XA_HEADS, XA_HEAD_DIM), lambda i, j: (layer, i, 0, 0, 0))
    else:
        kv_spec = pl.BlockSpec((None, None, N_MEM, d), lambda i, j: (layer, i, 0, 0))
    return pl.pallas_call(
        functools.partial(_attn_kernel, heads_minor=heads_minor),
        grid=(b, t // tq),
        in_specs=[pl.BlockSpec((1, tq, d), lambda i, j: (i, j, 0)), kv_spec, kv_spec],
        out_specs=pl.BlockSpec((1, tq, d), lambda i, j: (i, j, 0)),
        out_shape=jax.ShapeDtypeStruct((b, t, d), F32),
        compiler_params=_params("parallel", "arbitrary"),
        name="mem_attn",
    )(q, mk, mv)


def _trunk(x, mem_k, mem_v, shift0, wkv0, conv0, P, *, pad_t):
    b, t, d = x.shape
    m = b * t
    depth = P["norms"].shape[0]
    x = x.reshape(m, d)
    shifts, wkvs, convs, vrows = [], [], [], []

    def pad_rows(a):
        a = a.reshape(b, t, a.shape[-1])
        return a if pad_t == t else jnp.pad(a, ((0, 0), (0, pad_t - t), (0, 0)))

    def unpad_rows(a):
        return a[:, :t].reshape(m, a.shape[-1])

    for l in range(depth):
        n = P["norms"][l]
        x = _ffn(x, n[0], P["f1_wg"], P["f1_wu"], P["f1_wd"], l)
        if l % 2 == 0:
            e = l // 2
            p = _mm(x, P["w_in_even"], e, gain=n[1], tm=512, tn=(PROJ_A + PROJ_B) // 2)
            prm = {key: P[key][e] for key in ("shift_mu", "decay_w0", "decay_w2", "iclr_a0", "iclr_a2",
                                              "gate_g2", "k_k", "k_a", "r_k", "lnx_w", "lnx_b", "conv_w")}
            chunk = min(WKV_CHUNK, pad_t)
            y, sh, s, cb = _even_mixer(pad_rows(p), shift0[e], wkv0[e], conv0[e], prm,
                                       chunk=chunk, valid=min(chunk, t), nb=1 if t > chunk else SUBLANES)
            x = _mm(unpad_rows(y), P["w_out_even"], e, res=x, res_scale=1.0)
            shifts.append(sh)
            wkvs.append(s)
            convs.append(cb)
        else:
            o = l // 2
            p = _mm(x, P["w_in_odd"], o, gain=n[1], act="gelu")
            period = min(GM_CHUNK, t)
            reps = GM_CHUNK // period
            ws = jnp.tile(P["gm_ws"][o][:, :period, :period], (1, reps, reps))
            bs = jnp.tile(P["gm_bs"][o][:, :period], (1, reps)).T
            yc, vr = _gmlp(p, P["gm_ln_w"][o], P["gm_ln_b"][o], ws, bs, period=period)
            x = _mm(yc, P["w_out_odd"], o, res=x, res_scale=1.0)
            vrows.append(vr.reshape(b, t, GM_WIDTH))
        q = _mm(x, P["xa_wq"], l, gain=n[2])
        att = _attn(pad_rows(q), mem_k, mem_v, l)
        x = _mm(unpad_rows(att), P["xa_wo"], l, res=x, res_scale=1.0)
        x = _ffn(x, n[3], P["f2_wg"], P["f2_wu"], P["f2_wd"], l,
                 final_gain=P["final_norm"] if l == depth - 1 else None)
    return x.reshape(b, t, d), shifts, wkvs, convs, vrows


def kernel(x_prompt, x_sample, mem_prompt, state_shift, state_wkv, state_conv, cache_mem_k, cache_mem_v,
           norms, final_norm, f1_wg, f1_wu, f1_wd, f2_wg, f2_wu, f2_wd, xa_wq, xa_wk, xa_wv, xa_wo,
           w_in_even, w_out_even, shift_mu, decay_w0, decay_w2, iclr_a0, iclr_a2, gate_g2, k_k, k_a, r_k,
           lnx_w, lnx_b, conv_w, w_in_odd, w_out_odd, gm_ln_w, gm_ln_b, gm_ws, gm_bs):
    P = dict(norms=norms, final_norm=final_norm, f1_wg=f1_wg, f1_wu=f1_wu, f1_wd=f1_wd,
             f2_wg=f2_wg, f2_wu=f2_wu, f2_wd=f2_wd, xa_wq=xa_wq, xa_wo=xa_wo,
             w_in_even=w_in_even, w_out_even=w_out_even, shift_mu=shift_mu, decay_w0=decay_w0,
             decay_w2=decay_w2, iclr_a0=iclr_a0, iclr_a2=iclr_a2, gate_g2=gate_g2, k_k=k_k, k_a=k_a,
             r_k=r_k, lnx_w=lnx_w, lnx_b=lnx_b, conv_w=conv_w, w_in_odd=w_in_odd, w_out_odd=w_out_odd,
             gm_ln_w=gm_ln_w, gm_ln_b=gm_ln_b, gm_ws=gm_ws, gm_bs=gm_bs)
    depth = norms.shape[0]
    n_even = w_in_even.shape[0]
    bp, tp, d = x_prompt.shape
    bs_, ts, _ = x_sample.shape

    mem = mem_prompt.reshape(bp * N_MEM, d)
    new_k = jnp.stack([_mm(mem, xa_wk, l) for l in range(depth)]).reshape(depth, bp, N_MEM, d)
    new_v = jnp.stack([_mm(mem, xa_wv, l) for l in range(depth)]).reshape(depth, bp, N_MEM, d)
    new_mem_k_p = new_k.reshape(depth, bp, N_MEM, XA_HEADS, XA_HEAD_DIM)
    new_mem_v_p = new_v.reshape(depth, bp, N_MEM, XA_HEADS, XA_HEAD_DIM)

    shift0 = jnp.zeros((n_even, bp, PROJ_A), F32)
    wkv0 = jnp.zeros((n_even, bp, N_HEADS_A, HEAD_A, HEAD_A), F32)
    conv0 = jnp.zeros((n_even, bp, 2, MIX_B), F32)
    y_prompt, p_sh, p_wkv, p_cv, _ = _trunk(x_prompt, new_k, new_v, shift0, wkv0, conv0, P, pad_t=tp)
    pad_ts = -(-ts // SUBLANES) * SUBLANES
    y_sample, s_sh, s_wkv, s_cv, s_v = _trunk(x_sample, cache_mem_k, cache_mem_v, state_shift, state_wkv,
                                              state_conv, P, pad_t=pad_ts)
    return (y_prompt, y_sample, jnp.stack(p_sh), jnp.stack(p_wkv), jnp.stack(p_cv), new_mem_k_p, new_mem_v_p,
            jnp.stack(s_sh), jnp.stack(s_wkv), jnp.stack(s_cv), jnp.stack(s_v))
```

```python
import functools
import math

import jax
import jax.numpy as jnp
from jax import lax
from jax.experimental import pallas as pl
from jax.experimental.pallas import tpu as pltpu

F32 = jnp.float32
BF16 = jnp.bfloat16

D_MODEL = 1024
D_FF = 2816
MIX_A = 512
HEAD_A = 64
N_HEADS_A = MIX_A // HEAD_A
LORA_DECAY = 64
LORA_ICLR = 64
LORA_GATE = 128
PROJ_A = 3 * MIX_A + LORA_DECAY + LORA_ICLR + LORA_GATE
MIX_B = 512
PROJ_B = 3 * MIX_B
GM_WIDTH = 1024
GM_GROUPS = 8
GM_GROUP_DIM = GM_WIDTH // GM_GROUPS
GM_CHUNK = 128
N_MEM = 256
XA_HEADS = 4
XA_HEAD_DIM = D_MODEL // XA_HEADS
RMS_EPS = 1e-6
LN_EPS = 1e-5
GN_EPS = HEAD_A * 1e-5

V7X_VMEM_BYTES = 64 * 1024 * 1024
VMEM_LIMIT_BYTES = V7X_VMEM_BYTES - 8 * 1024 * 1024
SUBLANES = 8

WKV_CHUNK = 64
FFN_TILE_F = 256


def _params(*semantics):
    return pltpu.CompilerParams(dimension_semantics=semantics, vmem_limit_bytes=VMEM_LIMIT_BYTES)


def _rms(x, gain):
    return x * lax.rsqrt(jnp.mean(x * x, axis=-1, keepdims=True) + RMS_EPS) * gain


def _dot(a, b):
    return jnp.dot(a.astype(BF16), b.astype(BF16), preferred_element_type=F32)


def _mm_kernel(*refs, norm, act, res_scale):
    it = iter(refs)
    x_ref = next(it)
    g_ref = next(it) if norm else None
    w_ref = next(it)
    r_ref = next(it) if res_scale is not None else None
    o_ref = next(it)
    xs_ref = next(it)

    @pl.when(pl.program_id(1) == 0)
    def _():
        x = x_ref[...]
        if norm:
            x = _rms(x, g_ref[...])
        xs_ref[...] = x.astype(BF16)

    acc = jnp.dot(xs_ref[...], w_ref[...].astype(BF16), preferred_element_type=F32)
    if act == "gelu":
        acc = 0.5 * acc * (1.0 + lax.erf(acc * math.sqrt(0.5)))
    if res_scale is not None:
        acc = r_ref[...] + res_scale * acc
    o_ref[...] = acc


def _mm(x, w, layer, *, gain=None, act=None, res=None, res_scale=None, tm=1024, tn=1024):
    m, k = x.shape
    n = w.shape[2]
    tm = min(tm, m)
    tn = min(tn, n)
    assert m % tm == 0 and n % tn == 0
    norm = gain is not None
    if res is None:
        res_scale = None
    in_specs = [pl.BlockSpec((tm, k), lambda i, j: (i, 0))]
    args = [x]
    if norm:
        in_specs.append(pl.BlockSpec((1, k), lambda i, j: (0, 0)))
        args.append(gain.reshape(1, k))
    in_specs.append(pl.BlockSpec((None, k, tn), lambda i, j: (layer, 0, j)))
    args.append(w)
    if res is not None:
        in_specs.append(pl.BlockSpec((tm, tn), lambda i, j: (i, j)))
        args.append(res)
    return pl.pallas_call(
        functools.partial(_mm_kernel, norm=norm, act=act, res_scale=res_scale),
        grid=(m // tm, n // tn),
        in_specs=in_specs,
        out_specs=pl.BlockSpec((tm, tn), lambda i, j: (i, j)),
        out_shape=jax.ShapeDtypeStruct((m, n), F32),
        scratch_shapes=[pltpu.VMEM((tm, k), BF16)],
        compiler_params=_params("parallel", "arbitrary"),
        name="proj",
    )(*args)


def _ffn_kernel(*refs, final_norm):
    it = iter(refs)
    x_ref, g_ref, wg_ref, wu_ref, wd_ref = (next(it) for _ in range(5))
    fg_ref = next(it) if final_norm else None
    o_ref, xs_ref, h_ref = next(it), next(it), next(it)
    tf = wg_ref.shape[1]
    n_f = h_ref.shape[1] // tf
    j = pl.program_id(1)

    @pl.when(j == 0)
    def _():
        xs_ref[...] = _rms(x_ref[...], g_ref[...]).astype(BF16)

    @pl.when(j < n_f)
    def _():
        xs = xs_ref[...]
        gate = jnp.dot(xs, wg_ref[...].astype(BF16), preferred_element_type=F32)
        up = jnp.dot(xs, wu_ref[...].astype(BF16), preferred_element_type=F32)
        h = (gate * jax.nn.sigmoid(gate)) * up
        h_ref[:, pl.ds(pl.multiple_of(j * tf, tf), tf)] = h.astype(BF16)

    @pl.when(j == n_f)
    def _():
        y = x_ref[...] + 0.5 * jnp.dot(h_ref[...], wd_ref[...].astype(BF16), preferred_element_type=F32)
        if final_norm:
            y = _rms(y, fg_ref[...])
        o_ref[...] = y


def _ffn(x, gain, wg, wu, wd, layer, *, final_gain=None, tm=1024):
    m, d = x.shape
    f = wg.shape[2]
    tm = min(tm, m)
    tf = FFN_TILE_F
    assert m % tm == 0 and f % tf == 0
    n_f = f // tf
    final_norm = final_gain is not None
    in_specs = [
        pl.BlockSpec((tm, d), lambda i, j: (i, 0)),
        pl.BlockSpec((1, d), lambda i, j: (0, 0)),
        pl.BlockSpec((None, d, tf), lambda i, j: (layer, 0, jnp.minimum(j, n_f - 1))),
        pl.BlockSpec((None, d, tf), lambda i, j: (layer, 0, jnp.minimum(j, n_f - 1))),
        pl.BlockSpec((None, f, d), lambda i, j: (layer, 0, 0), pipeline_mode=pl.Buffered(1)),
    ]
    args = [x, gain.reshape(1, d), wg, wu, wd]
    if final_norm:
        in_specs.append(pl.BlockSpec((1, d), lambda i, j: (0, 0)))
        args.append(final_gain.reshape(1, d))
    return pl.pallas_call(
        functools.partial(_ffn_kernel, final_norm=final_norm),
        grid=(m // tm, n_f + 1),
        in_specs=in_specs,
        out_specs=pl.BlockSpec((tm, d), lambda i, j: (i, 0)),
        out_shape=jax.ShapeDtypeStruct((m, d), F32),
        scratch_shapes=[pltpu.VMEM((tm, d), BF16), pltpu.VMEM((tm, f), BF16)],
        compiler_params=_params("parallel", "arbitrary"),
        name="ffn",
    )(*args)


NN = ((1,), (0,))
NT = ((1,), (1,))
TN = ((0,), (0,))


def _dotg(a, b, dims):
    return lax.dot_general(a.astype(BF16), b.astype(BF16), (dims, ((), ())), preferred_element_type=F32)


def _cumsum_rows(x, rows):
    row = lax.broadcasted_iota(jnp.int32, (rows, 1), 0)
    d = 1
    while d < rows:
        x = x + jnp.where(row >= d, pltpu.roll(x, d, 0), 0.0)
        d *= 2
    return x


def _even_mixer_kernel(p_ref, shift0_ref, s0_ref, conv0_ref, mu_ref, w0_ref, w2_ref, a0_ref, a2_ref,
                       g2_ref, kk_ref, ka_ref, rk_ref, lnw_ref, lnb_ref, cw_ref,
                       y_ref, shift_out_ref, s_out_ref, conv_out_ref,
                       shift_carry, conv_carry, *, chunk, valid, nb):
    @pl.when(pl.program_id(1) == 0)
    def _():
        shift_carry[...] = shift0_ref[...]
        conv_carry[...] = conv0_ref[...]
        s_out_ref[...] = s0_ref[...]

    row = lax.broadcasted_iota(jnp.int32, (chunk, 1), 0)
    live = row < valid
    ti = lax.broadcasted_iota(jnp.int32, (chunk, chunk), 0)
    si = lax.broadcasted_iota(jnp.int32, (chunk, chunk), 1)
    incl = si <= ti
    strict = si < ti
    o1, o2, o3 = MIX_A, 2 * MIX_A, 3 * MIX_A
    o4 = o3 + LORA_DECAY
    o5 = o4 + LORA_ICLR

    def one_sequence(bi, carry):
        p = p_ref[bi]

        pa = p[:, :PROJ_A]
        prev = jnp.where(row == 0, shift_carry[bi], pltpu.roll(pa, 1, 0))
        ps = pa + mu_ref[...] * (prev - pa)
        last = pa[valid - 1:valid, :]
        shift_carry[bi] = last
        shift_out_ref[bi] = last

        r, k, v = ps[:, :o1], ps[:, o1:o2], ps[:, o2:o3]
        wd, ad, gd = ps[:, o3:o4], ps[:, o4:o5], ps[:, o5:]
        z = -(w0_ref[...] + _dot(jnp.tanh(wd), w2_ref[...]))
        softplus = jnp.maximum(z, 0.0) + jnp.log1p(jnp.exp(-jnp.abs(z)))
        log_decay = -jnp.exp(-softplus - 0.5)
        iclr = jax.nn.sigmoid(a0_ref[...] + _dot(ad, a2_ref[...]))
        gate = _dot(jax.nn.sigmoid(gd), g2_ref[...])
        kkv = k * kk_ref[...]
        k = k * (1.0 + (iclr - 1.0) * ka_ref[...])
        if valid < chunk:
            log_decay = jnp.where(live, log_decay, 0.0)
            k = jnp.where(live, k, 0.0)
            v = jnp.where(live, v, 0.0)
            kkv = jnp.where(live, kkv, 0.0)

        cum = _cumsum_rows(log_decay, chunk)
        cum_end = cum[chunk - 1:chunk, :]
        e_in = jnp.exp(cum)
        e_in_prev = jnp.exp(cum - log_decay)
        e_out = jnp.exp(-cum)
        e_tail = jnp.exp(cum_end - cum)
        e_end = jnp.exp(cum_end)
        r_in = r * e_in
        k_out = k * e_out
        k_tail = k * e_tail
        rk_bonus = r * k * rk_ref[...]

        heads = range(N_HEADS_A)
        sls = [slice(h * HEAD_A, (h + 1) * HEAD_A) for h in heads]
        s0 = [s_out_ref[bi, h] for h in heads]
        v_h, ar, b_out, k_out_h, tails = [], [], [], [], []
        for sl in sls:
            kk_h = kkv[:, sl]
            kk_h = kk_h / jnp.maximum(jnp.sqrt(jnp.sum(kk_h * kk_h, axis=-1, keepdims=True)), 1e-12)
            b_h = kk_h * iclr[:, sl]
            v_h.append(v[:, sl])
            ar.append(jnp.concatenate([-kk_h * e_in_prev[:, sl], r_in[:, sl]], axis=0))
            b_out.append(b_h * e_out[:, sl])
            k_out_h.append(k_out[:, sl])
            tails.append(jnp.concatenate([b_h * e_tail[:, sl], k_tail[:, sl]], axis=0))
        mask2 = jnp.concatenate([strict, incl], axis=0)
        g_b = [jnp.where(mask2, _dotg(ar[h], b_out[h], NT), 0.0) for h in heads]
        g_k = [jnp.where(mask2, _dotg(ar[h], k_out_h[h], NT), 0.0) for h in heads]
        from_state = [_dotg(ar[h], s0[h], NT) for h in heads]
        from_v = [_dotg(g_k[h], v_h[h], NN) for h in heads]
        sa = [from_state[h][:chunk] + from_v[h][:chunk] for h in heads]
        n_pow = [g_b[h][:chunk] for h in heads]
        span = 1
        while span < chunk:
            sa = [sa[h] + _dotg(n_pow[h], sa[h], NN) for h in heads]
            span *= 2
            if span < chunk:
                n_pow = [_dotg(n_pow[h], n_pow[h], NN) for h in heads]
        ys = [from_state[h][chunk:] + from_v[h][chunk:] + _dotg(g_b[h][chunk:], sa[h], NN) for h in heads]
        new_states = [s0[h] * e_end[:, sls[h]]
                      + _dotg(jnp.concatenate([sa[h], v_h[h]], axis=0), tails[h], TN) for h in heads]
        for h in heads:
            s_out_ref[bi, h] = new_states[h]
        for h, sl in enumerate(sls):
            y = ys[h]
            mean = jnp.mean(y, axis=-1, keepdims=True)
            var = jnp.mean(jnp.square(y - mean), axis=-1, keepdims=True)
            y = (y - mean) * lax.rsqrt(var + GN_EPS) * lnw_ref[:, sl] + lnb_ref[:, sl]
            y = y + jnp.sum(rk_bonus[:, sl], axis=-1, keepdims=True) * v_h[h]
            y_ref[bi, :, sl] = y * gate[:, sl]

        pb = p[:, PROJ_A:]
        hb, bg, cg = pb[:, :MIX_B], pb[:, MIX_B:2 * MIX_B], pb[:, 2 * MIX_B:]
        zc = cg * hb
        buf = conv_carry[bi]
        buf0, buf1 = buf[0:1, :], buf[1:2, :]
        z1 = jnp.where(row == 0, buf1, pltpu.roll(zc, 1, 0))
        z2 = jnp.where(row == 0, buf0, jnp.where(row == 1, buf1, pltpu.roll(zc, 2, 0)))
        y_ref[bi, :, MIX_A:] = bg * (cw_ref[0:1, :] * z2 + cw_ref[1:2, :] * z1 + cw_ref[2:3, :] * zc)
        tail = zc[valid - 2:valid, :]
        conv_carry[bi] = tail
        conv_out_ref[bi] = tail
        return carry

    if nb == 1:
        one_sequence(0, 0)
    else:
        lax.fori_loop(0, nb, one_sequence, 0)


def _even_mixer(p, shift0, s0, conv0, prm, *, chunk, valid, nb):
    b, t, width = p.shape
    assert b % nb == 0 and t % chunk == 0
    row2 = lambda a: a.reshape(1, -1)
    small = [row2(prm["shift_mu"]), row2(prm["decay_w0"]), prm["decay_w2"], row2(prm["iclr_a0"]),
             prm["iclr_a2"], prm["gate_g2"], row2(prm["k_k"]), row2(prm["k_a"]), row2(prm["r_k"]),
             row2(prm["lnx_w"]), row2(prm["lnx_b"]), prm["conv_w"]]
    const = lambda a: pl.BlockSpec(a.shape, lambda i, c: (0,) * a.ndim)
    in_specs = [
        pl.BlockSpec((nb, chunk, width), lambda i, c: (i, c, 0)),
        pl.BlockSpec((nb, 1, PROJ_A), lambda i, c: (i, 0, 0)),
        pl.BlockSpec((nb, N_HEADS_A, HEAD_A, HEAD_A), lambda i, c: (i, 0, 0, 0)),
        pl.BlockSpec((nb, 2, MIX_B), lambda i, c: (i, 0, 0)),
    ] + [const(a) for a in small]
    out_specs = [
        pl.BlockSpec((nb, chunk, D_MODEL), lambda i, c: (i, c, 0)),
        pl.BlockSpec((nb, 1, PROJ_A), lambda i, c: (i, 0, 0)),
        pl.BlockSpec((nb, N_HEADS_A, HEAD_A, HEAD_A), lambda i, c: (i, 0, 0, 0)),
        pl.BlockSpec((nb, 2, MIX_B), lambda i, c: (i, 0, 0)),
    ]
    out_shape = [
        jax.ShapeDtypeStruct((b, t, D_MODEL), F32),
        jax.ShapeDtypeStruct((b, 1, PROJ_A), F32),
        jax.ShapeDtypeStruct((b, N_HEADS_A, HEAD_A, HEAD_A), F32),
        jax.ShapeDtypeStruct((b, 2, MIX_B), F32),
    ]
    y, shift, state, conv = pl.pallas_call(
        functools.partial(_even_mixer_kernel, chunk=chunk, valid=valid, nb=nb),
        grid=(b // nb, t // chunk),
        in_specs=in_specs,
        out_specs=out_specs,
        out_shape=out_shape,
        scratch_shapes=[pltpu.VMEM((nb, 1, PROJ_A), F32), pltpu.VMEM((nb, 2, MIX_B), F32)],
        compiler_params=_params("parallel", "arbitrary"),
        name="even_mixer",
    )(p, shift0.reshape(b, 1, PROJ_A), s0, conv0, *small)
    return y, shift.reshape(b, PROJ_A), state, conv


def _gmlp_kernel(p_ref, lnw_ref, lnb_ref, ws_ref, bs_ref, y_ref, v_ref, *, rows, period):
    p = p_ref[...]
    u = p[:, :GM_WIDTH]
    v = p[:, GM_WIDTH:]
    mean = jnp.mean(v, axis=-1, keepdims=True)
    var = jnp.mean(jnp.square(v - mean), axis=-1, keepdims=True)
    v = (v - mean) * lax.rsqrt(var + LN_EPS) * lnw_ref[...] + lnb_ref[...]
    v_ref[...] = v
    ti = lax.broadcasted_iota(jnp.int32, (rows, rows), 0)
    si = lax.broadcasted_iota(jnp.int32, (rows, rows), 1)
    causal = (si <= ti) & (si >= ti - ti % period)
    vb = v.astype(BF16)
    for g in range(GM_GROUPS):
        sl = slice(g * GM_GROUP_DIM, (g + 1) * GM_GROUP_DIM)
        ws = jnp.where(causal, ws_ref[g], 0.0).astype(BF16)
        f = jnp.dot(ws, vb[:, sl], preferred_element_type=F32) + bs_ref[:, g:g + 1]
        y_ref[:, sl] = u[:, sl] * f


def _gmlp(p, ln_w, ln_b, ws, bs, *, period):
    m = p.shape[0]
    rows = ws.shape[1]
    assert m % rows == 0 and rows % period == 0
    y, v = pl.pallas_call(
        functools.partial(_gmlp_kernel, rows=rows, period=period),
        grid=(m // rows,),
        in_specs=[
            pl.BlockSpec((rows, 2 * GM_WIDTH), lambda i: (i, 0)),
            pl.BlockSpec((1, GM_WIDTH), lambda i: (0, 0)),
            pl.BlockSpec((1, GM_WIDTH), lambda i: (0, 0)),
            pl.BlockSpec((GM_GROUPS, rows, rows), lambda i: (0, 0, 0)),
            pl.BlockSpec((rows, GM_GROUPS), lambda i: (0, 0)),
        ],
        out_specs=[pl.BlockSpec((rows, GM_WIDTH), lambda i: (i, 0)),
                   pl.BlockSpec((rows, GM_WIDTH), lambda i: (i, 0))],
        out_shape=[jax.ShapeDtypeStruct((m, GM_WIDTH), F32), jax.ShapeDtypeStruct((m, GM_WIDTH), F32)],
        compiler_params=_params("parallel"),
        name="gmlp",
    )(p, ln_w.reshape(1, GM_WIDTH), ln_b.reshape(1, GM_WIDTH), ws, bs)
    return y, v


def _attn_head(q_h, k_h, v_h):
    sc = lax.dot_general(q_h, k_h.astype(BF16), (((1,), (1,)), ((), ())),
                         preferred_element_type=F32) * (XA_HEAD_DIM ** -0.5)
    e = jnp.exp(sc - jnp.max(sc, axis=-1, keepdims=True))
    pr = e / jnp.sum(e, axis=-1, keepdims=True)
    return jnp.dot(pr.astype(BF16), v_h.astype(BF16), preferred_element_type=F32)


def _attn_kernel(q_ref, k_ref, v_ref, o_ref):
    q = q_ref[0].astype(BF16)
    for h in range(XA_HEADS):
        sl = slice(h * XA_HEAD_DIM, (h + 1) * XA_HEAD_DIM)
        o_ref[0, :, sl] = _attn_head(q[:, sl], k_ref[:, sl], v_ref[:, sl])


def _attn(q, mk, mv, layer, *, tq=512):
    b, t, d = q.shape
    tq = min(tq, t)
    assert t % tq == 0
    kv_spec = pl.BlockSpec((None, None, N_MEM, d), lambda i, j: (layer, i, 0, 0))
    return pl.pallas_call(
        _attn_kernel,
        grid=(b, t // tq),
        in_specs=[pl.BlockSpec((1, tq, d), lambda i, j: (i, j, 0)), kv_spec, kv_spec],
        out_specs=pl.BlockSpec((1, tq, d), lambda i, j: (i, j, 0)),
        out_shape=jax.ShapeDtypeStruct((b, t, d), F32),
        compiler_params=_params("parallel", "arbitrary"),
        name="mem_attn",
    )(q, mk, mv)


def _attn_cache_kernel(q_ref, k_hbm, v_hbm, o_ref, k_buf, v_buf, sem, *, layer):
    i = pl.program_id(0)

    def head_copies(seq, slot):
        return [pltpu.make_async_copy(src.at[layer, seq, :, h, :], buf.at[slot, h], sem.at[slot, which, h])
                for which, (src, buf) in enumerate(((k_hbm, k_buf), (v_hbm, v_buf)))
                for h in range(XA_HEADS)]

    @pl.when(i == 0)
    def _():
        for c in head_copies(0, 0):
            c.start()

    @pl.when(i + 1 < pl.num_programs(0))
    def _():
        for c in head_copies(i + 1, (i + 1) % 2):
            c.start()

    slot = i % 2
    for c in head_copies(i, slot):
        c.wait()
    q = q_ref[0].astype(BF16)
    for h in range(XA_HEADS):
        sl = slice(h * XA_HEAD_DIM, (h + 1) * XA_HEAD_DIM)
        o_ref[0, :, sl] = _attn_head(q[:, sl], k_buf[slot, h], v_buf[slot, h])


def _attn_cache(q, mk, mv, layer):
    b, t, d = q.shape
    slab = pltpu.VMEM((2, XA_HEADS, N_MEM, XA_HEAD_DIM), F32)
    return pl.pallas_call(
        functools.partial(_attn_cache_kernel, layer=layer),
        grid=(b,),
        in_specs=[pl.BlockSpec((1, t, d), lambda i: (i, 0, 0)),
                  pl.BlockSpec(memory_space=pl.ANY), pl.BlockSpec(memory_space=pl.ANY)],
        out_specs=pl.BlockSpec((1, t, d), lambda i: (i, 0, 0)),
        out_shape=jax.ShapeDtypeStruct((b, t, d), F32),
        scratch_shapes=[slab, slab, pltpu.SemaphoreType.DMA((2, 2, XA_HEADS))],
        compiler_params=_params("arbitrary"),
        name="mem_attn_cache",
    )(q, mk, mv)


def _trunk(x, mem_k, mem_v, shift0, wkv0, conv0, P, *, pad_t):
    b, t, d = x.shape
    m = b * t
    depth = P["norms"].shape[0]
    x = x.reshape(m, d)
    shifts, wkvs, convs, vrows = [], [], [], []

    def pad_rows(a):
        a = a.reshape(b, t, a.shape[-1])
        return a if pad_t == t else jnp.pad(a, ((0, 0), (0, pad_t - t), (0, 0)))

    def unpad_rows(a):
        return a[:, :t].reshape(m, a.shape[-1])

    for l in range(depth):
        n = P["norms"][l]
        x = _ffn(x, n[0], P["f1_wg"], P["f1_wu"], P["f1_wd"], l)
        if l % 2 == 0:
            e = l // 2
            p = _mm(x, P["w_in_even"], e, gain=n[1], tn=(PROJ_A + PROJ_B) // 2)
            prm = {key: P[key][e] for key in ("shift_mu", "decay_w0", "decay_w2", "iclr_a0", "iclr_a2",
                                              "gate_g2", "k_k", "k_a", "r_k", "lnx_w", "lnx_b", "conv_w")}
            chunk = min(WKV_CHUNK, pad_t)
            y, sh, s, cb = _even_mixer(pad_rows(p), shift0[e], wkv0[e], conv0[e], prm,
                                       chunk=chunk, valid=min(chunk, t), nb=1 if t > chunk else SUBLANES)
            x = _mm(unpad_rows(y), P["w_out_even"], e, res=x, res_scale=1.0)
            shifts.append(sh)
            wkvs.append(s)
            convs.append(cb)
        else:
            o = l // 2
            p = _mm(x, P["w_in_odd"], o, gain=n[1], act="gelu")
            period = min(GM_CHUNK, t)
            reps = GM_CHUNK // period
            ws = jnp.tile(P["gm_ws"][o][:, :period, :period], (1, reps, reps))
            bs = jnp.tile(P["gm_bs"][o][:, :period], (1, reps)).T
            yc, vr = _gmlp(p, P["gm_ln_w"][o], P["gm_ln_b"][o], ws, bs, period=period)
            x = _mm(yc, P["w_out_odd"], o, res=x, res_scale=1.0)
            vrows.append(vr.reshape(b, t, GM_WIDTH))
        q = _mm(x, P["xa_wq"], l, gain=n[2])
        att = (_attn_cache if mem_k.ndim == 5 else _attn)(pad_rows(q), mem_k, mem_v, l)
        x = _mm(unpad_rows(att), P["xa_wo"], l, res=x, res_scale=1.0)
        x = _ffn(x, n[3], P["f2_wg"], P["f2_wu"], P["f2_wd"], l,
                 final_gain=P["final_norm"] if l == depth - 1 else None)
    return x.reshape(b, t, d), shifts, wkvs, convs, vrows


def kernel(x_prompt, x_sample, mem_prompt, state_shift, state_wkv, state_conv, cache_mem_k, cache_mem_v,
           norms, final_norm, f1_wg, f1_wu, f1_wd, f2_wg, f2_wu, f2_wd, xa_wq, xa_wk, xa_wv, xa_wo,
           w_in_even, w_out_even, shift_mu, decay_w0, decay_w2, iclr_a0, iclr_a2, gate_g2, k_k, k_a, r_k,
           lnx_w, lnx_b, conv_w, w_in_odd, w_out_odd, gm_ln_w, gm_ln_b, gm_ws, gm_bs):
    P = dict(norms=norms, final_norm=final_norm, f1_wg=f1_wg, f1_wu=f1_wu, f1_wd=f1_wd,
             f2_wg=f2_wg, f2_wu=f2_wu, f2_wd=f2_wd, xa_wq=xa_wq, xa_wo=xa_wo,
             w_in_even=w_in_even, w_out_even=w_out_even, shift_mu=shift_mu, decay_w0=decay_w0,
             decay_w2=decay_w2, iclr_a0=iclr_a0, iclr_a2=iclr_a2, gate_g2=gate_g2, k_k=k_k, k_a=k_a,
             r_k=r_k, lnx_w=lnx_w, lnx_b=lnx_b, conv_w=conv_w, w_in_odd=w_in_odd, w_out_odd=w_out_odd,
             gm_ln_w=gm_ln_w, gm_ln_b=gm_ln_b, gm_ws=gm_ws, gm_bs=gm_bs)
    depth = norms.shape[0]
    n_even = w_in_even.shape[0]
    bp, tp, d = x_prompt.shape
    bs_, ts, _ = x_sample.shape

    mem = mem_prompt.reshape(bp * N_MEM, d)
    new_k = jnp.stack([_mm(mem, xa_wk, l) for l in range(depth)]).reshape(depth, bp, N_MEM, d)
    new_v = jnp.stack([_mm(mem, xa_wv, l) for l in range(depth)]).reshape(depth, bp, N_MEM, d)
    new_mem_k_p = new_k.reshape(depth, bp, N_MEM, XA_HEADS, XA_HEAD_DIM)
    new_mem_v_p = new_v.reshape(depth, bp, N_MEM, XA_HEADS, XA_HEAD_DIM)

    shift0 = jnp.zeros((n_even, bp, PROJ_A), F32)
    wkv0 = jnp.zeros((n_even, bp, N_HEADS_A, HEAD_A, HEAD_A), F32)
    conv0 = jnp.zeros((n_even, bp, 2, MIX_B), F32)
    y_prompt, p_sh, p_wkv, p_cv, _ = _trunk(x_prompt, new_k, new_v, shift0, wkv0, conv0, P, pad_t=tp)
    pad_ts = -(-ts // SUBLANES) * SUBLANES
    y_sample, s_sh, s_wkv, s_cv, s_v = _trunk(x_sample, cache_mem_k, cache_mem_v, state_shift, state_wkv,
                                              state_conv, P, pad_t=pad_ts)
    return (y_prompt, y_sample, jnp.stack(p_sh), jnp.stack(p_wkv), jnp.stack(p_cv), new_mem_k_p, new_mem_v_p,
            jnp.stack(s_sh), jnp.stack(s_wkv), jnp.stack(s_cv), jnp.stack(s_v))
```
